```python
import jax, jax.numpy as jnp
from jax import lax
import numpy as np

D_MODEL = 1024
BATCH = 4
SEQ = 4096
DEPTH = 4

GRID_W = 64
CTX_LEN = 256
M_HEADS = 4
M_HEAD_DIM = 128
M_WIDTH = M_HEADS * M_HEAD_DIM
M_CHUNK = 128
M_CONV = 5
MLA_HEADS = 8
MLA_Q_RANK = 256
MLA_KV_RANK = 128
MLA_NOPE = 64
MLA_ROPE = 32
MLA_V = 64
MLA_QK = MLA_NOPE + MLA_ROPE
G_HEADS = 8
G_KV_HEADS = 2
G_HEAD_DIM = 64
N_BRANCH = 3
BRANCH_W = 512
D_FF = 2816
Q_BLOCK = 128
ROPE_THETA = 10000.0
EPS = 1e-6
IN_SPLITS = (M_WIDTH, M_WIDTH, M_WIDTH, M_WIDTH, 4 * M_HEADS,
             MLA_Q_RANK, MLA_KV_RANK, MLA_ROPE,
             G_HEADS * G_HEAD_DIM, G_KV_HEADS * G_HEAD_DIM, G_KV_HEADS * G_HEAD_DIM,
             N_BRANCH * D_MODEL)
D_IN = sum(IN_SPLITS)

kernel_name = 'hybrid_mlstm_mla_gqa_prefix_block'


def _rms(x, g):
    xf = x.astype(jnp.float32)
    y = xf * lax.rsqrt(jnp.mean(xf * xf, axis=-1, keepdims=True) + EPS)
    return (y * g.astype(jnp.float32)).astype(x.dtype)


def _modulate(x, g, shift, scale):
    return _rms(x, g) * (1.0 + scale) + shift


def _swiglu(h, w_in, w_out):
    a, b = jnp.split(h @ w_in, 2, axis=-1)
    return (jax.nn.silu(a) * b) @ w_out


def _split_in(z):
    offs = np.cumsum(IN_SPLITS)[:-1].tolist()
    return jnp.split(z, offs, axis=-1)


def _dwconv(x, w):
    pad = w.shape[0] // 2
    return lax.conv_general_dilated(x, w, (1,), [(pad, pad)],
                                    dimension_numbers=('NWC', 'WIO', 'NWC'),
                                    feature_group_count=x.shape[-1])


def _rope_tables(rows, cols, rot_dim):
    quarter = rot_dim // 4
    inv = ROPE_THETA ** (-jnp.arange(quarter, dtype=jnp.float32) / quarter)
    ang = jnp.concatenate([rows[:, None] * inv, cols[:, None] * inv], axis=-1)
    return jnp.cos(ang), jnp.sin(ang)


def _rope(x, cos, sin):
    r = 2 * cos.shape[-1]
    d = x.shape[-1]
    xp, x1, x2 = x[..., :d - r], x[..., d - r:d - r // 2], x[..., d - r // 2:]
    c = cos[None, :, None, :].astype(x.dtype)
    s = sin[None, :, None, :].astype(x.dtype)
    return jnp.concatenate([xp, x1 * c - x2 * s, x2 * c + x1 * s], axis=-1)


def _attend(q, k, v):
    B, T, KH, G, dk = q.shape
    nb = T // Q_BLOCK
    qb = jnp.moveaxis(q.reshape(B, nb, Q_BLOCK, KH, G, dk), 1, 0)
    scale = dk ** -0.5

    def block(qi):
        s = jnp.einsum('bqhgd,bkhd->bhgqk', qi, k).astype(jnp.float32) * scale
        p = jax.nn.softmax(s, axis=-1).astype(v.dtype)
        return jnp.einsum('bhgqk,bkhe->bqhge', p, v)

    o = lax.map(block, qb)
    return jnp.moveaxis(o, 0, 1).reshape(B, T, KH * G * v.shape[-1])


def _mlstm_scan(q, k, v, ig, flog, state):
    B, T, H, dk = q.shape
    nc = T // M_CHUNK
    f32 = jnp.float32

    def chunks(a):
        return jnp.moveaxis(a.astype(f32).reshape((B, nc, M_CHUNK) + a.shape[2:]), 1, 0)

    lower = jnp.tril(jnp.ones((M_CHUNK, M_CHUNK), dtype=bool))[None, :, :, None]

    def step(carry, xs):
        C, n, m = carry
        qc, kc, vc, ic, fc = xs
        b = jnp.cumsum(fc, axis=1)
        dmat = jnp.where(lower, b[:, :, None, :] - b[:, None, :, :] + ic[:, None, :, :], -jnp.inf)
        m_inter = b + m[:, None, :]
        m_t = jnp.maximum(m_inter, jnp.max(dmat, axis=2))
        s = jnp.einsum('bthd,bshd->btsh', qc, kc) * jnp.exp(dmat - m_t[:, :, None, :])
        a = jnp.exp(m_inter - m_t)
        num = jnp.einsum('btsh,bshe->bthe', s, vc) + a[..., None] * jnp.einsum('bthd,bhde->bthe', qc, C)
        den = jnp.sum(s, axis=2) + a * jnp.einsum('bthd,bhd->bth', qc, n)
        h = num / jnp.maximum(jnp.abs(den), jnp.exp(-m_t))[..., None]
        b_end = b[:, -1, :]
        w_log = b_end[:, None, :] - b + ic
        m_new = jnp.maximum(b_end + m, jnp.max(w_log, axis=1))
        decay = jnp.exp(b_end + m - m_new)
        w = jnp.exp(w_log - m_new[:, None, :])
        C = decay[..., None, None] * C + jnp.einsum('bsh,bshd,bshe->bhde', w, kc, vc)
        n = decay[..., None] * n + jnp.einsum('bsh,bshd->bhd', w, kc)
        return (C, n, m_new), h

    state, hs = lax.scan(step, state, (chunks(q), chunks(k), chunks(v), chunks(ig), chunks(flog)))
    h = jnp.moveaxis(hs, 0, 1).reshape(B, T, H, v.shape[-1]).astype(v.dtype)
    return h, state


def _mlstm_prep(q, k, v, gates, w_conv, b_gate):
    B, T, _ = q.shape
    qk = jax.nn.silu(_dwconv(jnp.concatenate([q, k], axis=-1), w_conv))
    q, k = jnp.split(qk, 2, axis=-1)
    q = q.reshape(B, T, M_HEADS, M_HEAD_DIM)
    k = k.reshape(B, T, M_HEADS, M_HEAD_DIM) * (M_HEAD_DIM ** -0.5)
    v = v.reshape(B, T, M_HEADS, M_HEAD_DIM)
    gt = gates.reshape(B, T, 4, M_HEADS).astype(jnp.float32) + b_gate.astype(jnp.float32)
    fwd = (gt[:, :, 0], jax.nn.log_sigmoid(gt[:, :, 1]))
    bwd = (gt[:, :, 2], jax.nn.log_sigmoid(gt[:, :, 3]))
    return q, k, v, fwd, bwd


def _mlstm_out(h, o, g_head):
    B, T = o.shape[:2]
    return _rms(h, g_head.reshape(M_HEADS, M_HEAD_DIM)).reshape(B, T, M_WIDTH) * jax.nn.sigmoid(o)


def _mlstm_branch(pl, pc, w_conv, b_gate, g_head, ctx_out):
    ql, kl, vl, fl, bl = _mlstm_prep(pl[0], pl[1], pl[2], pl[4], w_conv, b_gate)
    qc, kc, vc, fc, bc = _mlstm_prep(pc[0], pc[1], pc[2], pc[4], w_conv, b_gate)
    B = ql.shape[0]
    f32 = jnp.float32
    zero = (jnp.zeros((B, M_HEADS, M_HEAD_DIM, M_HEAD_DIM), f32),
            jnp.zeros((B, M_HEADS, M_HEAD_DIM), f32),
            jnp.zeros((B, M_HEADS), f32))

    def flip(a):
        return jnp.flip(a, axis=1)

    hcf, s_f = _mlstm_scan(qc, kc, vc, fc[0], fc[1], zero)
    hlf, _ = _mlstm_scan(ql, kl, vl, fl[0], fl[1], s_f)
    hcb, s_b = _mlstm_scan(flip(qc), flip(kc), flip(vc), flip(bc[0]), flip(bc[1]), zero)
    hlb, _ = _mlstm_scan(flip(ql), flip(kl), flip(vl), flip(bl[0]), flip(bl[1]), s_b)
    y = _mlstm_out(hlf + flip(hlb), pl[3], g_head)
    yc = _mlstm_out(hcf + flip(hcb), pc[3], g_head) if ctx_out else None
    return y, yc


def _mla_qkv(cq, ckv, kr, g_cq, g_ckv, w_uq, w_ukv, g_q, g_k):
    B, T, _ = cq.shape
    q = (_rms(cq, g_cq) @ w_uq).reshape(B, T, MLA_HEADS, MLA_QK)
    kv = (_rms(ckv, g_ckv) @ w_ukv).reshape(B, T, MLA_HEADS, MLA_NOPE + MLA_V)
    k_rope = jnp.broadcast_to(kr[:, :, None, :], (B, T, MLA_HEADS, MLA_ROPE))
    k = jnp.concatenate([kv[..., :MLA_NOPE], k_rope], axis=-1)
    return _rms(q, g_q), _rms(k, g_k), kv[..., MLA_NOPE:]


def _gqa_qkv(q, k, v, g_q, g_k):
    B, T, _ = q.shape
    q = _rms(q.reshape(B, T, G_HEADS, G_HEAD_DIM), g_q)
    k = _rms(k.reshape(B, T, G_KV_HEADS, G_HEAD_DIM), g_k)
    return q, k, v.reshape(B, T, G_KV_HEADS, G_HEAD_DIM)


def _group(q, n_kv):
    B, T, H, d = q.shape
    return q.reshape(B, T, n_kv, H // n_kv, d)


def _merge(ys, gate_pre, w_branch, w_out):
    B, T, _ = gate_pre.shape
    y = jnp.stack(ys, axis=2)
    proj = jnp.einsum('btnw,nwd->btnd', y, w_branch)
    g = jax.nn.sigmoid(gate_pre.reshape(B, T, N_BRANCH, D_MODEL))
    return jnp.einsum('btnd,btnd->btd', g, proj) @ w_out


def _token_mixer(h, hc, w_in, b_gate, w_conv, g_head, g_cq, g_ckv, w_uq, w_ukv, g_qm, g_km, g_qg, g_kg,
                 w_branch, w_out, rope_mla, rope_gqa, ctx_out):
    pl = _split_in(h @ w_in)
    pc = _split_in(hc @ w_in)
    y_m, yc_m = _mlstm_branch(pl, pc, w_conv, b_gate, g_head, ctx_out)
    qa, ka, va = _mla_qkv(pl[5], pl[6], pl[7], g_cq, g_ckv, w_uq, w_ukv, g_qm, g_km)
    qa_c, ka_c, va_c = _mla_qkv(pc[5], pc[6], pc[7], g_cq, g_ckv, w_uq, w_ukv, g_qm, g_km)
    qa = _rope(qa, *rope_mla)
    ka = _rope(ka, *rope_mla)
    y_a = _attend(qa[:, :, :, None, :], jnp.concatenate([ka, ka_c], axis=1), jnp.concatenate([va, va_c], axis=1))
    qg, kg, vg = _gqa_qkv(pl[8], pl[9], pl[10], g_qg, g_kg)
    qg_c, kg_c, vg_c = _gqa_qkv(pc[8], pc[9], pc[10], g_qg, g_kg)
    qg = _rope(qg, *rope_gqa)
    kg = _rope(kg, *rope_gqa)
    y_g = _attend(_group(qg, G_KV_HEADS), jnp.concatenate([kg, kg_c], axis=1), jnp.concatenate([vg, vg_c], axis=1))
    y = _merge((y_m, y_a, y_g), pl[11], w_branch, w_out)
    if not ctx_out:
        return y, None
    yc_a = _attend(qa_c[:, :, :, None, :], ka_c, va_c)
    yc_g = _attend(_group(qg_c, G_KV_HEADS), kg_c, vg_c)
    yc = _merge((yc_m, yc_a, yc_g), pc[11], w_branch, w_out)
    return y, yc


def setup_inputs(seed: int = 0) -> dict:
    key = jax.random.key(seed)
    ks = jax.random.split(key, 24)
    f32 = jnp.float32
    D = D_MODEL

    def nrm(k, shape, scale):
        return jax.random.normal(k, shape, f32) * scale

    def gain(k, shape):
        return 1.0 + 0.02 * jax.random.normal(k, shape, f32)

    gk = jax.random.split(ks[8], 4)
    b_mgate = jnp.stack([
        nrm(gk[0], (DEPTH, M_HEADS), 0.1),
        jax.random.uniform(gk[1], (DEPTH, M_HEADS), f32, 3.0, 6.0),
        nrm(gk[2], (DEPTH, M_HEADS), 0.1),
        jax.random.uniform(gk[3], (DEPTH, M_HEADS), f32, 3.0, 6.0),
    ], axis=1)
    return {
        'x': nrm(ks[0], (BATCH, SEQ, D), 1.0),
        'c': nrm(ks[1], (BATCH, D), 1.0),
        'ctx': nrm(ks[2], (BATCH, CTX_LEN, D), 1.0),
        'c_ctx': nrm(ks[3], (D,), 1.0),
        'w_mod': nrm(ks[4], (DEPTH, D, 9 * D), 0.5 * D ** -0.5),
        'b_mod': nrm(ks[5], (DEPTH, 9 * D), 0.01),
        'g_norm': gain(ks[6], (DEPTH, 3, D)),
        'w_ffn_in': nrm(ks[7], (DEPTH, 2, D, 2 * D_FF), D ** -0.5),
        'w_ffn_out': nrm(ks[9], (DEPTH, 2, D_FF, D), D_FF ** -0.5),
        'w_in': nrm(ks[10], (DEPTH, D, D_IN), D ** -0.5),
        'b_mgate': b_mgate,
        'w_conv': nrm(ks[11], (DEPTH, M_CONV, 1, 2 * M_WIDTH), M_CONV ** -0.5),
        'g_mhead': gain(ks[12], (DEPTH, M_WIDTH)),
        'g_cq': gain(ks[13], (DEPTH, MLA_Q_RANK)),
        'g_ckv': gain(ks[14], (DEPTH, MLA_KV_RANK)),
        'w_uq': nrm(ks[15], (DEPTH, MLA_Q_RANK, MLA_HEADS * MLA_QK), MLA_Q_RANK ** -0.5),
        'w_ukv': nrm(ks[16], (DEPTH, MLA_KV_RANK, MLA_HEADS * (MLA_NOPE + MLA_V)), MLA_KV_RANK ** -0.5),
        'g_q_mla': gain(ks[17], (DEPTH, MLA_QK)),
        'g_k_mla': gain(ks[18], (DEPTH, MLA_QK)),
        'g_q_gqa': gain(ks[19], (DEPTH, G_HEAD_DIM)),
        'g_k_gqa': gain(ks[20], (DEPTH, G_HEAD_DIM)),
        'w_branch': nrm(ks[21], (DEPTH, N_BRANCH, BRANCH_W, D), BRANCH_W ** -0.5),
        'w_out': nrm(ks[22], (DEPTH, D, D), D ** -0.5),
    }


def reference(x, c, ctx, c_ctx, w_mod, b_mod, g_norm, w_ffn_in, w_ffn_out, w_in, b_mgate, w_conv, g_mhead,
              g_cq, g_ckv, w_uq, w_ukv, g_q_mla, g_k_mla, g_q_gqa, g_k_gqa, w_branch, w_out):
    n_tok = x.shape[1]
    ROWS = n_tok // GRID_W
    rows = jnp.broadcast_to(jnp.arange(ROWS, dtype=jnp.float32)[:, None], (ROWS, GRID_W)).reshape(-1)
    cols = jnp.broadcast_to(jnp.arange(GRID_W, dtype=jnp.float32)[None, :], (ROWS, GRID_W)).reshape(-1)
    rope_mla = _rope_tables(rows, cols, MLA_ROPE)
    rope_gqa = _rope_tables(rows, cols, G_HEAD_DIM)
    s_lat = jax.nn.silu(c)
    s_ctx = jax.nn.silu(c_ctx)
    xc = ctx
    for l in range(DEPTH):
        last = l == DEPTH - 1
        mod = jnp.split((s_lat @ w_mod[l] + b_mod[l])[:, None, :], 9, axis=-1)
        modc = jnp.split((s_ctx @ w_mod[l] + b_mod[l])[None, None, :], 9, axis=-1)
        x = x + 0.5 * mod[2] * _swiglu(_modulate(x, g_norm[l, 0], mod[0], mod[1]), w_ffn_in[l, 0], w_ffn_out[l, 0])
        xc = xc + 0.5 * modc[2] * _swiglu(_modulate(xc, g_norm[l, 0], modc[0], modc[1]), w_ffn_in[l, 0], w_ffn_out[l, 0])
        h = _modulate(x, g_norm[l, 1], mod[3], mod[4])
        hc = _modulate(xc, g_norm[l, 1], modc[3], modc[4])
        y, yc = _token_mixer(h, hc, w_in[l], b_mgate[l], w_conv[l], g_mhead[l], g_cq[l], g_ckv[l], w_uq[l], w_ukv[l],
                             g_q_mla[l], g_k_mla[l], g_q_gqa[l], g_k_gqa[l], w_branch[l], w_out[l],
                             rope_mla, rope_gqa, not last)
        x = x + mod[5] * y
        if not last:
            xc = xc + modc[5] * yc
            xc = xc + 0.5 * modc[8] * _swiglu(_modulate(xc, g_norm[l, 2], modc[6], modc[7]), w_ffn_in[l, 1], w_ffn_out[l, 1])
        x = x + 0.5 * mod[8] * _swiglu(_modulate(x, g_norm[l, 2], mod[6], mod[7]), w_ffn_in[l, 1], w_ffn_out[l, 1])
    return x
```

```python
import functools
import math

import jax
import jax.numpy as jnp
from jax import lax
from jax.experimental import pallas as pl
from jax.experimental.pallas import tpu as pltpu

D = 1024
SEQ = 4096
CTX = 256
NT = SEQ + CTX
DEPTH = 4
GRID_W = 64
MH = 4
MW = 512
MCONV = 5
CHUNK = 128
AH = 8
A_QR = 256
A_KVR = 128
A_NOPE = 64
A_ROPE = 32
A_V = 64
A_QK = A_NOPE + A_ROPE
GH = 8
GKV = 2
GD = 64
DFF = 2816
ROPE_THETA = 10000.0
EPS = 1e-6
LOG2E = 1.4426950408889634

TM = 256
NBLK = NT // TM
LAT_BLKS = SEQ // TM
HALO = 8
LANE = 128
VMEM_LIMIT = 56 * 1024 * 1024

F32 = jnp.float32
BF16 = jnp.bfloat16

O_V, O_O, O_G, O_CQ, O_CKV, O_KR, O_GQ, O_GK, O_GV, W2_COLS = 0, 512, 1024, 1152, 1408, 1536, 1664, 2688, 2944, 3200


def _sigmoid(x):
    return 1.0 / (1.0 + jnp.exp(-x))


def _log_sigmoid(x):
    return jnp.minimum(x, 0.0) - jnp.log1p(jnp.exp(-jnp.abs(x)))


def _rms_mod(x, g, shift, scale):
    ms = jnp.mean(x * x, axis=-1, keepdims=True)
    return (x * lax.rsqrt(ms + EPS) * g) * (1.0 + scale) + shift


def _mod_part(mod_ref, j):
    return mod_ref[0][:, j * D:(j + 1) * D]


def _mod_row(b, i):
    return jnp.where(i == LAT_BLKS, 4, b)


def _mod_kernel(s_ref, w_ref, b_ref, o_ref):
    s = s_ref[...]
    s = (s * _sigmoid(s)).astype(BF16)
    o_ref[0] = jnp.dot(s, w_ref[0].astype(BF16), preferred_element_type=F32) + b_ref[0]


def _mod_call(cond, w_mod, b_mod):
    tn = 1024
    return pl.pallas_call(
        _mod_kernel,
        out_shape=jax.ShapeDtypeStruct((DEPTH, 8, 9 * D), F32),
        grid=(DEPTH, 9 * D // tn),
        in_specs=[
            pl.BlockSpec((8, D), lambda l, j: (0, 0)),
            pl.BlockSpec((1, D, tn), lambda l, j: (l, 0, j)),
            pl.BlockSpec((1, 1, tn), lambda l, j: (l, 0, j)),
        ],
        out_specs=pl.BlockSpec((1, 8, tn), lambda l, j: (l, 0, j)),
        compiler_params=pltpu.CompilerParams(dimension_semantics=("arbitrary", "arbitrary"),
                                             vmem_limit_bytes=VMEM_LIMIT),
        name="adaln_mod",
    )(cond, w_mod, b_mod.reshape(DEPTH, 1, 9 * D))


def _ffn_kernel(x_ref, mod_ref, g_ref, wi_ref, wo_ref, o_ref, *, off):
    x = x_ref[0]
    h = _rms_mod(x, g_ref[...], _mod_part(mod_ref, off), _mod_part(mod_ref, off + 1)).astype(BF16)
    ab = jnp.dot(h, wi_ref[...], preferred_element_type=F32)
    a = ab[:, :DFF]
    b = ab[:, DFF:]
    u = (a * _sigmoid(a) * b).astype(BF16)
    y = jnp.dot(u, wo_ref[...], preferred_element_type=F32)
    o_ref[0] = x + 0.5 * _mod_part(mod_ref, off + 2) * y


def _ffn_call(x, mod, g, wi, wo, off, nblk):
    B = x.shape[0]
    return pl.pallas_call(
        functools.partial(_ffn_kernel, off=off),
        out_shape=jax.ShapeDtypeStruct((B, nblk * TM, D), F32),
        grid=(B, nblk),
        in_specs=[
            pl.BlockSpec((1, TM, D), lambda b, i: (b, i, 0)),
            pl.BlockSpec((1, 1, 9 * D), lambda b, i: (_mod_row(b, i), 0, 0)),
            pl.BlockSpec((1, D), lambda b, i: (0, 0)),
            pl.BlockSpec((D, 2 * DFF), lambda b, i: (0, 0), pipeline_mode=pl.Buffered(1)),
            pl.BlockSpec((DFF, D), lambda b, i: (0, 0), pipeline_mode=pl.Buffered(1)),
        ],
        out_specs=pl.BlockSpec((1, TM, D), lambda b, i: (b, i, 0)),
        compiler_params=pltpu.CompilerParams(dimension_semantics=("arbitrary", "arbitrary"),
                                             vmem_limit_bytes=VMEM_LIMIT),
        name="ffn",
    )(x, mod, g, wi, wo)


def _rope(x, c, s, lane, split, up, down):
    rot = jnp.where(lane < split, pltpu.roll(x, up, 1), pltpu.roll(x, down, 1))
    return x * c + rot * s


def _head_norm(x, g, n):
    ms = jnp.sum(x * x, axis=-1, keepdims=True) * (1.0 / n)
    return x * lax.rsqrt(ms + EPS) * g


def _mixer_in_kernel(x_ref, xp_ref, xn_ref, mod_ref, g_ref, w1_ref, w2_ref, wgt_ref, bg_ref, bgt_ref, wconv_ref,
                     gcq_ref, gckv_ref, wuq_ref, wukv_ref, gqa_ref, gka_ref, gqg_ref, gkg_ref,
                     ca_ref, sa_ref, cg_ref, sg_ref,
                     qm_ref, km_ref, vm_ref, om_ref, gm_ref, gmt_ref,
                     qa_ref, ka_ref, va_ref, qg_ref, kg_ref, vg_ref,
                     zqk_scr):
    i = pl.program_id(1)
    shift = _mod_part(mod_ref, 3)
    scale = _mod_part(mod_ref, 4)
    x_ext = jnp.concatenate([xp_ref[0], x_ref[0], xn_ref[0]], axis=0)
    h_ext = _rms_mod(x_ext, g_ref[...], shift, scale)
    h = h_ext[HALO:HALO + TM].astype(BF16)

    zqk_scr[...] = jnp.dot(h_ext.astype(BF16), w1_ref[...], preferred_element_type=F32)

    @pl.when((i == 0) | (i == LAT_BLKS))
    def _():
        zqk_scr[0:HALO, :] = jnp.zeros((HALO, 2 * MW), F32)

    @pl.when((i == LAT_BLKS - 1) | (i == LAT_BLKS))
    def _():
        zqk_scr[HALO + TM:, :] = jnp.zeros((HALO, 2 * MW), F32)

    wconv = wconv_ref[...]
    acc = None
    for j in range(MCONV):
        term = wconv[j:j + 1, :] * zqk_scr[pl.ds(HALO - MCONV // 2 + j, TM), :]
        acc = term if acc is None else acc + term
    qk = acc * _sigmoid(acc)
    qm_ref[0] = qk[:, :MW]
    km_ref[0] = qk[:, MW:] * (CHUNK ** -0.5)

    z2 = jnp.dot(h, w2_ref[...], preferred_element_type=F32)
    vm_ref[0] = z2[:, O_V:O_V + MW]
    om_ref[0] = z2[:, O_O:O_O + MW]

    gates = z2[:, O_G:O_G + LANE] + bg_ref[...]
    gl = lax.broadcasted_iota(jnp.int32, gates.shape, 1)
    gm_ref[0] = jnp.where((gl & 4) != 0, _log_sigmoid(gates), gates)
    gt = lax.dot_general(wgt_ref[...], h, (((1,), (1,)), ((), ())), preferred_element_type=F32) + bgt_ref[...]
    gr = lax.broadcasted_iota(jnp.int32, gt.shape, 0)
    gmt_ref[0] = jnp.where((gr & 4) != 0, _log_sigmoid(gt), gt)

    lane = lax.broadcasted_iota(jnp.int32, (TM, LANE), 1)

    cq = z2[:, O_CQ:O_CQ + A_QR]
    cqn = (cq * lax.rsqrt(jnp.mean(cq * cq, axis=-1, keepdims=True) + EPS) * gcq_ref[...]).astype(BF16)
    qa_all = jnp.dot(cqn, wuq_ref[...], preferred_element_type=F32)
    ckv = z2[:, O_CKV:O_CKV + A_KVR]
    ckvn = (ckv * lax.rsqrt(jnp.mean(ckv * ckv, axis=-1, keepdims=True) + EPS) * gckv_ref[...]).astype(BF16)
    kv = jnp.dot(ckvn, wukv_ref[...], preferred_element_type=F32)
    kr = z2[:, O_KR:O_KR + LANE]
    ca, sa = ca_ref[...], sa_ref[...]
    gqa, gka = gqa_ref[...], gka_ref[...]
    a_split = A_NOPE + A_ROPE // 2
    for hd in range(AH):
        qh = _head_norm(qa_all[:, hd * LANE:(hd + 1) * LANE], gqa, A_QK)
        qh = _rope(qh, ca, sa, lane, a_split, LANE - A_ROPE // 2, A_ROPE // 2)
        qa_ref[0, hd] = (qh * (A_QK ** -0.5 * LOG2E)).astype(BF16)
        kh = _head_norm(kv[:, hd * LANE:(hd + 1) * LANE] + kr, gka, A_QK)
        kh = _rope(kh, ca, sa, lane, a_split, LANE - A_ROPE // 2, A_ROPE // 2)
        ka_ref[0, hd] = kh.astype(BF16)
    for p in range(AH // 2):
        va_ref[0, p] = kv[:, AH * LANE + p * LANE:AH * LANE + (p + 1) * LANE].astype(BF16)

    cg, sg = cg_ref[...], sg_ref[...]
    gqg, gkg = gqg_ref[...], gkg_ref[...]
    for hd in range(GH):
        qh = _head_norm(z2[:, O_GQ + hd * LANE:O_GQ + (hd + 1) * LANE], gqg, GD)
        qh = _rope(qh, cg, sg, lane, GD // 2, LANE - GD // 2, GD // 2)
        qg_ref[0, hd] = (qh * (GD ** -0.5 * LOG2E)).astype(BF16)
    for g in range(GKV):
        kh = _head_norm(z2[:, O_GK + g * LANE:O_GK + (g + 1) * LANE], gkg, GD)
        kh = _rope(kh, cg, sg, lane, GD // 2, LANE - GD // 2, GD // 2)
        kg_ref[0, g] = kh.astype(BF16)
        vg_ref[0, g] = z2[:, O_GV + g * LANE:O_GV + (g + 1) * LANE].astype(BF16)


def _mixer_in_call(x, mod, g, lw, tabs):
    B = x.shape[0]
    nh8 = NT // HALO
    const2 = lambda b, i: (0, 0)
    tok3 = lambda b, i: (b, i, 0)
    head4 = lambda b, i: (b, 0, i, 0)

    def wspec(shape):
        return pl.BlockSpec(shape, const2, pipeline_mode=pl.Buffered(1))

    in_specs = [
        pl.BlockSpec((1, TM, D), tok3),
        pl.BlockSpec((1, HALO, D), lambda b, i: (b, jnp.maximum(i * (TM // HALO) - 1, 0), 0)),
        pl.BlockSpec((1, HALO, D), lambda b, i: (b, jnp.minimum((i + 1) * (TM // HALO), nh8 - 1), 0)),
        pl.BlockSpec((1, 1, 9 * D), lambda b, i: (_mod_row(b, i), 0, 0)),
        pl.BlockSpec((1, D), const2),
        wspec((D, 2 * MW)),
        wspec((D, W2_COLS)),
        wspec((4 * MH, D)),
        pl.BlockSpec((1, LANE), const2),
        pl.BlockSpec((4 * MH, TM), const2),
        pl.BlockSpec((8, 2 * MW), const2),
        pl.BlockSpec((1, A_QR), const2),
        pl.BlockSpec((1, A_KVR), const2),
        wspec((A_QR, AH * LANE)),
        wspec((A_KVR, AH * LANE + AH * A_V)),
        pl.BlockSpec((1, LANE), const2),
        pl.BlockSpec((1, LANE), const2),
        pl.BlockSpec((1, LANE), const2),
        pl.BlockSpec((1, LANE), const2),
        pl.BlockSpec((TM, LANE), lambda b, i: (i, 0)),
        pl.BlockSpec((TM, LANE), lambda b, i: (i, 0)),
        pl.BlockSpec((TM, LANE), lambda b, i: (i, 0)),
        pl.BlockSpec((TM, LANE), lambda b, i: (i, 0)),
    ]
    out_shape = [
        jax.ShapeDtypeStruct((B, NT, MW), F32),
        jax.ShapeDtypeStruct((B, NT, MW), F32),
        jax.ShapeDtypeStruct((B, NT, MW), F32),
        jax.ShapeDtypeStruct((B, NT, MW), F32),
        jax.ShapeDtypeStruct((B, NT, LANE), F32),
        jax.ShapeDtypeStruct((B, 4 * MH, NT), F32),
        jax.ShapeDtypeStruct((B, AH, NT, LANE), BF16),
        jax.ShapeDtypeStruct((B, AH, NT, LANE), BF16),
        jax.ShapeDtypeStruct((B, AH // 2, NT, LANE), BF16),
        jax.ShapeDtypeStruct((B, GH, NT, LANE), BF16),
        jax.ShapeDtypeStruct((B, GKV, NT, LANE), BF16),
        jax.ShapeDtypeStruct((B, GKV, NT, LANE), BF16),
    ]
    out_specs = [
        pl.BlockSpec((1, TM, MW), tok3),
        pl.BlockSpec((1, TM, MW), tok3),
        pl.BlockSpec((1, TM, MW), tok3),
        pl.BlockSpec((1, TM, MW), tok3),
        pl.BlockSpec((1, TM, LANE), tok3),
        pl.BlockSpec((1, 4 * MH, TM), lambda b, i: (b, 0, i)),
        pl.BlockSpec((1, AH, TM, LANE), head4),
        pl.BlockSpec((1, AH, TM, LANE), head4),
        pl.BlockSpec((1, AH // 2, TM, LANE), head4),
        pl.BlockSpec((1, GH, TM, LANE), head4),
        pl.BlockSpec((1, GKV, TM, LANE), head4),
        pl.BlockSpec((1, GKV, TM, LANE), head4),
    ]
    return pl.pallas_call(
        _mixer_in_kernel,
        out_shape=out_shape,
        grid=(B, NBLK),
        in_specs=in_specs,
        out_specs=out_specs,
        scratch_shapes=[pltpu.VMEM((TM + 2 * HALO, 2 * MW), F32)],
        compiler_params=pltpu.CompilerParams(dimension_semantics=("arbitrary", "arbitrary"),
                                             vmem_limit_bytes=VMEM_LIMIT),
        name="mixer_in",
    )(x, x, x, mod, g, lw["w1"], lw["w2"], lw["wgt"], lw["bg"], lw["bgt"], lw["wconv"],
      lw["gcq"], lw["gckv"], lw["wuq"], lw["wukv"], lw["gqa"], lw["gka"], lw["gqg"], lw["gkg"],
      tabs["ca"], tabs["sa"], tabs["cg"], tabs["sg"])


def _mlstm_kernel(q_ref, k_ref, v_ref, g_ref, gt_ref, *rest, rev, final):
    if final:
        hf_ref, o_ref, gh_ref, y_ref, c_scr, n_scr, m_scr = rest
    else:
        h_ref, c_scr, n_scr, m_scr = rest
    step = pl.program_id(1)

    @pl.when(step == 0)
    def _():
        c_scr[...] = jnp.zeros(c_scr.shape, F32)
        n_scr[...] = jnp.zeros(n_scr.shape, F32)
        m_scr[...] = jnp.zeros(m_scr.shape, F32)

    L = CHUNK
    r = lax.broadcasted_iota(jnp.int32, (L, L), 0)
    c = lax.broadcasted_iota(jnp.int32, (L, L), 1)
    mask = (c >= r) if rev else (c <= r)
    tri_col = mask.astype(F32)
    tri_row = ((r >= c) if rev else (r <= c)).astype(F32)
    g = g_ref[0]
    gt = gt_ref[0]
    bc = jnp.dot(tri_col, g, preferred_element_type=F32, precision=lax.Precision.HIGHEST)
    br = jnp.dot(gt, tri_row, preferred_element_type=F32, precision=lax.Precision.HIGHEST)
    end = 0 if rev else L - 1
    d = 1 if rev else 0
    for hd in range(MH):
        li = d * 2 * MH + hd
        lf = li + MH
        i_col = g[:, li:li + 1]
        b_col = bc[:, lf:lf + 1]
        i_row = gt[li:li + 1, :]
        b_row = br[lf:lf + 1, :]
        b_end = b_col[end:end + 1, :]
        m_prev = m_scr[hd][0:1, 0:1]
        dmat = jnp.where(mask, b_col - b_row + i_row, -1e30)
        m_inter = b_col + m_prev
        m_t = jnp.maximum(m_inter, jnp.max(dmat, axis=1, keepdims=True))
        p = jnp.exp(dmat - m_t)
        sl = slice(hd * LANE, (hd + 1) * LANE)
        q = q_ref[0][:, sl]
        k = k_ref[0][:, sl]
        v = v_ref[0][:, sl].astype(BF16)
        qb = q.astype(BF16)
        s = lax.dot_general(qb, k.astype(BF16), (((1,), (1,)), ((), ())), preferred_element_type=F32) * p
        a = jnp.exp(m_inter - m_t)
        cst = c_scr[hd]
        nst = n_scr[hd][0:1, :]
        num = (jnp.dot(s.astype(BF16), v, preferred_element_type=F32)
               + a * jnp.dot(qb, cst.astype(BF16), preferred_element_type=F32))
        den = jnp.sum(s, axis=1, keepdims=True) + a * jnp.sum(q * nst, axis=1, keepdims=True)
        h = num / jnp.maximum(jnp.abs(den), jnp.exp(-m_t))
        w_log = b_end - b_col + i_col
        m_new = jnp.maximum(b_end + m_prev, jnp.max(w_log, axis=0, keepdims=True))
        decay = jnp.exp(b_end + m_prev - m_new)
        kw = k * jnp.exp(w_log - m_new)
        c_scr[hd] = decay * cst + lax.dot_general(kw.astype(BF16), v, (((0,), (0,)), ((), ())),
                                                  preferred_element_type=F32)
        n_scr[hd] = jnp.broadcast_to(decay * nst + jnp.sum(kw, axis=0, keepdims=True), (8, LANE))
        m_scr[hd] = jnp.broadcast_to(m_new, (8, LANE))
        if final:
            ht = h + hf_ref[0][:, sl]
            hn = ht * lax.rsqrt(jnp.mean(ht * ht, axis=-1, keepdims=True) + EPS) * gh_ref[:, sl]
            y_ref[0, :, sl] = (hn * _sigmoid(o_ref[0][:, sl])).astype(y_ref.dtype)
        else:
            h_ref[0, :, sl] = h


def _mlstm_call(qm, km, vm, gm, gmt, rev, extra=None):
    B = qm.shape[0]
    nsteps = NT // CHUNK
    if rev:
        blk = lambda s: nsteps - 1 - s
    else:
        blk = lambda s: (s + SEQ // CHUNK) % nsteps
    tok = lambda b, s: (b, blk(s), 0)
    in_specs = [
        pl.BlockSpec((1, CHUNK, MW), tok),
        pl.BlockSpec((1, CHUNK, MW), tok),
        pl.BlockSpec((1, CHUNK, MW), tok),
        pl.BlockSpec((1, CHUNK, LANE), tok),
        pl.BlockSpec((1, 4 * MH, CHUNK), lambda b, s: (b, 0, blk(s))),
    ]
    args = [qm, km, vm, gm, gmt]
    final = extra is not None
    if final:
        hf, om, gh = extra
        in_specs += [pl.BlockSpec((1, CHUNK, MW), tok), pl.BlockSpec((1, CHUNK, MW), tok),
                     pl.BlockSpec((1, MW), lambda b, s: (0, 0))]
        args += [hf, om, gh]
        out_dtype = BF16
    else:
        out_dtype = F32
    return pl.pallas_call(
        functools.partial(_mlstm_kernel, rev=rev, final=final),
        out_shape=jax.ShapeDtypeStruct((B, NT, MW), out_dtype),
        grid=(B, nsteps),
        in_specs=in_specs,
        out_specs=pl.BlockSpec((1, CHUNK, MW), tok),
        scratch_shapes=[pltpu.VMEM((MH, LANE, LANE), F32), pltpu.VMEM((MH, 8, LANE), F32),
                        pltpu.VMEM((MH, 8, LANE), F32)],
        compiler_params=pltpu.CompilerParams(dimension_semantics=("arbitrary", "arbitrary"),
                                             vmem_limit_bytes=VMEM_LIMIT),
        name="mlstm_bwd" if rev else "mlstm_fwd",
    )(*args)


def _attn_kernel(q_ref, k_ref, v_ref, o_ref, *, kb):
    i = pl.program_id(2)
    lane = lax.broadcasted_iota(jnp.int32, (TM, LANE), 1)

    def run(k0, n):
        outs = []
        for j in range(2):
            q = q_ref[0, j]
            k = k_ref[0, j if kb == 2 else 0, pl.ds(k0, n), :]
            v = v_ref[0, 0, pl.ds(k0, n), :]
            s = lax.dot_general(q, k, (((1,), (1,)), ((), ())), preferred_element_type=F32)
            p = jnp.exp2(s - jnp.max(s, axis=-1, keepdims=True))
            l = jnp.sum(p, axis=-1, keepdims=True)
            outs.append(jnp.dot(p.astype(BF16), v, preferred_element_type=F32) / l)
        o_ref[0] = jnp.where(lane < GD, outs[0], outs[1]).astype(o_ref.dtype)

    @pl.when(i < LAT_BLKS)
    def _():
        run(0, NT)

    @pl.when(i == LAT_BLKS)
    def _():
        run(SEQ, CTX)


def _attn_call(q, k, v, kb, name):
    B, H = q.shape[0], q.shape[1]
    if kb == 2:
        kv_idx = lambda b, p, i: (b, p, 0, 0)
    else:
        kv_idx = lambda b, p, i: (b, p // (H // 2 // k.shape[1]), 0, 0)
    return pl.pallas_call(
        functools.partial(_attn_kernel, kb=kb),
        out_shape=jax.ShapeDtypeStruct((B, NT, H * GD), BF16),
        grid=(B, H // 2, NBLK),
        in_specs=[
            pl.BlockSpec((1, 2, TM, LANE), lambda b, p, i: (b, p, i, 0)),
            pl.BlockSpec((1, kb, NT, LANE), kv_idx),
            pl.BlockSpec((1, 1, NT, LANE), kv_idx),
        ],
        out_specs=pl.BlockSpec((1, TM, LANE), lambda b, p, i: (b, i, p)),
        compiler_params=pltpu.CompilerParams(dimension_semantics=("arbitrary", "arbitrary", "arbitrary"),
                                             vmem_limit_bytes=VMEM_LIMIT),
        name=name,
    )(q, k, v)


def _merge_kernel(x_ref, ym_ref, ya_ref, yg_ref, mod_ref, g_ref, wg_ref, wb_ref, wo_ref, o_ref):
    x = x_ref[0]
    h = _rms_mod(x, g_ref[...], _mod_part(mod_ref, 3), _mod_part(mod_ref, 4)).astype(BF16)
    gp = jnp.dot(h, wg_ref[...], preferred_element_type=F32)
    acc = None
    for n, y_ref in enumerate((ym_ref, ya_ref, yg_ref)):
        proj = jnp.dot(y_ref[0], wb_ref[n], preferred_element_type=F32)
        term = _sigmoid(gp[:, n * D:(n + 1) * D]) * proj
        acc = term if acc is None else acc + term
    y = jnp.dot(acc.astype(BF16), wo_ref[...], preferred_element_type=F32)
    o_ref[0] = x + _mod_part(mod_ref, 5) * y


def _merge_call(x, ym, ya, yg, mod, g, wg, wb, wo):
    B = x.shape[0]
    tok3 = lambda b, i: (b, i, 0)
    return pl.pallas_call(
        _merge_kernel,
        out_shape=jax.ShapeDtypeStruct((B, NT, D), F32),
        grid=(B, NBLK),
        in_specs=[
            pl.BlockSpec((1, TM, D), tok3),
            pl.BlockSpec((1, TM, MW), tok3),
            pl.BlockSpec((1, TM, MW), tok3),
            pl.BlockSpec((1, TM, MW), tok3),
            pl.BlockSpec((1, 1, 9 * D), lambda b, i: (_mod_row(b, i), 0, 0)),
            pl.BlockSpec((1, D), lambda b, i: (0, 0)),
            pl.BlockSpec((D, 3 * D), lambda b, i: (0, 0), pipeline_mode=pl.Buffered(1)),
            pl.BlockSpec((3, MW, D), lambda b, i: (0, 0, 0), pipeline_mode=pl.Buffered(1)),
            pl.BlockSpec((D, D), lambda b, i: (0, 0), pipeline_mode=pl.Buffered(1)),
        ],
        out_specs=pl.BlockSpec((1, TM, D), tok3),
        compiler_params=pltpu.CompilerParams(dimension_semantics=("arbitrary", "arbitrary"),
                                             vmem_limit_bytes=VMEM_LIMIT),
        name="merge",
    )(x, ym, ya, yg, mod, g, wg, wb, wo)


def _pad_heads(w, nh, hd, to=LANE, at=0):
    lead = w.shape[:-1]
    w = w.reshape(lead + (nh, hd))
    w = jnp.pad(w, [(0, 0)] * len(lead) + [(0, 0), (at, to - hd - at)])
    return w.reshape(lead + (nh * to,))


def _layer_weights(l, w_in, b_mgate, w_conv, g_cq, g_ckv, w_uq, w_ukv, g_q_mla, g_k_mla, g_q_gqa, g_k_gqa):
    w = w_in[l]
    o = 0
    parts = {}
    for name, n in (("q", MW), ("k", MW), ("v", MW), ("o", MW), ("g", 4 * MH), ("cq", A_QR), ("ckv", A_KVR),
                    ("kr", A_ROPE), ("gq", GH * GD), ("gk", GKV * GD), ("gv", GKV * GD), ("bg", 3 * D)):
        parts[name] = w[:, o:o + n]
        o += n
    gv = parts["gv"].reshape(D, GKV, 1, GD)
    w2 = jnp.concatenate([
        parts["v"], parts["o"],
        _pad_heads(parts["g"], 1, 4 * MH),
        parts["cq"], parts["ckv"],
        _pad_heads(parts["kr"], 1, A_ROPE, at=A_NOPE),
        _pad_heads(parts["gq"], GH, GD),
        _pad_heads(parts["gk"], GKV, GD),
        jnp.broadcast_to(gv, (D, GKV, 2, GD)).reshape(D, GKV * LANE),
    ], axis=1)
    ukv = w_ukv[l].reshape(A_KVR, AH, A_NOPE + A_V)
    wukv = jnp.concatenate([
        _pad_heads(ukv[:, :, :A_NOPE].reshape(A_KVR, AH * A_NOPE), AH, A_NOPE),
        ukv[:, :, A_NOPE:].reshape(A_KVR, AH * A_V),
    ], axis=1)
    bg = b_mgate[l].reshape(1, 4 * MH)
    return {
        "w1": jnp.concatenate([parts["q"], parts["k"]], axis=1).astype(BF16),
        "w2": w2.astype(BF16),
        "wgt": parts["g"].T.astype(BF16),
        "wgate": parts["bg"].astype(BF16),
        "bg": _pad_heads(bg, 1, 4 * MH),
        "bgt": jnp.broadcast_to(bg.reshape(4 * MH, 1), (4 * MH, TM)),
        "wconv": jnp.pad(w_conv[l].reshape(MCONV, 2 * MW), ((0, 8 - MCONV), (0, 0))),
        "gcq": g_cq[l].reshape(1, A_QR),
        "gckv": g_ckv[l].reshape(1, A_KVR),
        "wuq": _pad_heads(w_uq[l], AH, A_QK).astype(BF16),
        "wukv": wukv.astype(BF16),
        "gqa": _pad_heads(g_q_mla[l].reshape(1, A_QK), 1, A_QK),
        "gka": _pad_heads(g_k_mla[l].reshape(1, A_QK), 1, A_QK),
        "gqg": _pad_heads(g_q_gqa[l].reshape(1, GD), 1, GD),
        "gkg": _pad_heads(g_k_gqa[l].reshape(1, GD), 1, GD),
    }


def _rope_tables():
    pos = jnp.arange(SEQ, dtype=jnp.int32)
    rows = (pos // GRID_W).astype(F32)
    cols = (pos % GRID_W).astype(F32)

    def table(rot_dim, at):
        quarter = rot_dim // 4
        inv = ROPE_THETA ** (-jnp.arange(quarter, dtype=F32) / quarter)
        ang = jnp.concatenate([rows[:, None] * inv, cols[:, None] * inv], axis=-1)
        cos, sin = jnp.cos(ang), jnp.sin(ang)
        c = jnp.pad(jnp.concatenate([cos, cos], axis=-1), ((0, CTX), (at, LANE - rot_dim - at)),
                    constant_values=1.0)
        s = jnp.pad(jnp.concatenate([-sin, sin], axis=-1), ((0, CTX), (at, LANE - rot_dim - at)))
        c = c.at[SEQ:, :].set(1.0)
        return c, s

    ca, sa = table(A_ROPE, A_NOPE)
    cg, sg = table(GD, 0)
    return {"ca": ca, "sa": sa, "cg": cg, "sg": sg}


def kernel(x, c, ctx, c_ctx, w_mod, b_mod, g_norm, w_ffn_in, w_ffn_out, w_in, b_mgate, w_conv, g_mhead,
           g_cq, g_ckv, w_uq, w_ukv, g_q_mla, g_k_mla, g_q_gqa, g_k_gqa, w_branch, w_out):
    B = x.shape[0]
    tabs = _rope_tables()
    cond = jnp.concatenate([c, c_ctx[None, :], jnp.zeros((8 - B - 1, D), F32)], axis=0)
    mod_all = _mod_call(cond, w_mod, b_mod)
    xs = jnp.concatenate([x, ctx], axis=1)
    for l in range(DEPTH):
        last = l == DEPTH - 1
        mod = mod_all[l].reshape(8, 1, 9 * D)
        lw = _layer_weights(l, w_in, b_mgate, w_conv, g_cq, g_ckv, w_uq, w_ukv, g_q_mla, g_k_mla, g_q_gqa, g_k_gqa)
        wi = w_ffn_in[l].astype(BF16)
        wo = w_ffn_out[l].astype(BF16)
        xs = _ffn_call(xs, mod, g_norm[l, 0].reshape(1, D), wi[0], wo[0], 0, NBLK)
        g1 = g_norm[l, 1].reshape(1, D)
        qm, km, vm, om, gm, gmt, qa, ka, va, qg, kg, vg = _mixer_in_call(xs, mod, g1, lw, tabs)
        hf = _mlstm_call(qm, km, vm, gm, gmt, rev=False)
        ym = _mlstm_call(qm, km, vm, gm, gmt, rev=True, extra=(hf, om, g_mhead[l].reshape(1, MW)))
        ya = _attn_call(qa, ka, va, 2, "attn_mla")
        yg = _attn_call(qg, kg, vg, 1, "attn_gqa")
        xs = _merge_call(xs, ym, ya, yg, mod, g1, lw["wgate"], w_branch[l].astype(BF16), w_out[l].astype(BF16))
        xs = _ffn_call(xs, mod, g_norm[l, 2].reshape(1, D), wi[1], wo[1], 6, LAT_BLKS if last else NBLK)
    return xs
```
